```python
import math
import jax
import jax.numpy as jnp
from jax import lax
import numpy as np

D_MODEL = 1024
BATCH = 4
SEQ = 4096
DEPTH = 4

GRID_W = 64
NORM_EPS = 1e-6
ROPE_THETA = 10000.0
Q_BLOCK = 128

DN_HEADS = 6
DN_DK = 64
DN_DV = 64
DN_CHUNK = 64
DN_CONV_W = 5
GQA_HEADS = 6
GQA_KV_HEADS = 2
GQA_HD = 64
MLA_HEADS = 4
MLA_Q_RANK = 192
MLA_KV_RANK = 128
MLA_NOPE = 64
MLA_ROPE = 32
MLA_V = 64
N_GROUPS = 4
EXPERTS_PER_GROUP = 8
N_EXPERTS = N_GROUPS * EXPERTS_PER_GROUP
TOP_K_IN_GROUP = 2
D_EXPERT = 256
MOE_BLOCK = 128

DN_WIDTH = DN_HEADS * DN_DV
GQA_WIDTH = GQA_HEADS * GQA_HD
MLA_WIDTH = MLA_HEADS * MLA_V
D_MIX = DN_WIDTH + GQA_WIDTH + MLA_WIDTH
IN_SPLIT_SIZES = (2 * DN_HEADS * DN_DK + DN_WIDTH, DN_WIDTH, 2 * DN_HEADS, 2 * DN_HEADS,
                  GQA_WIDTH, GQA_KV_HEADS * GQA_HD, GQA_KV_HEADS * GQA_HD,
                  MLA_Q_RANK, MLA_KV_RANK, MLA_ROPE)
IN_COLS = sum(IN_SPLIT_SIZES)

kernel_name = "hybrid_parallel_heads_moe_encoder"


def rms_norm(x, gain):
    xf = x.astype(jnp.float32)
    y = xf * lax.rsqrt(jnp.mean(jnp.square(xf), axis=-1, keepdims=True) + NORM_EPS)
    return (y * gain.astype(jnp.float32)).astype(x.dtype)


def l2_normalize(x):
    xf = x.astype(jnp.float32)
    return (xf * lax.rsqrt(jnp.sum(jnp.square(xf), axis=-1, keepdims=True) + NORM_EPS)).astype(x.dtype)


def axial_rope_table(seq_len, rot_dim):
    rows = seq_len // GRID_W
    row = jnp.broadcast_to(jnp.arange(rows, dtype=jnp.float32)[:, None], (rows, GRID_W)).reshape(-1)
    col = jnp.broadcast_to(jnp.arange(GRID_W, dtype=jnp.float32)[None, :], (rows, GRID_W)).reshape(-1)
    n_freq = rot_dim // 4
    inv_freq = ROPE_THETA ** (-jnp.arange(n_freq, dtype=jnp.float32) / n_freq)
    ang = jnp.concatenate([row[:, None] * inv_freq, col[:, None] * inv_freq], axis=-1)
    return jnp.cos(ang), jnp.sin(ang)


def apply_rope(x, cos, sin):
    x1, x2 = jnp.split(x, 2, axis=-1)
    c = cos[:, None, :].astype(x.dtype)
    s = sin[:, None, :].astype(x.dtype)
    return jnp.concatenate([x1 * c - x2 * s, x1 * s + x2 * c], axis=-1)


def blocked_attention(q, k, v, scale):
    B, KH, G, S, Dk = q.shape
    Dv = v.shape[-1]
    nb = S // Q_BLOCK
    qb = jnp.moveaxis(q.reshape(B, KH, G, nb, Q_BLOCK, Dk), 3, 0)

    def attend(q_blk):
        s = jnp.einsum('bhgqd,bhkd->bhgqk', q_blk, k, preferred_element_type=jnp.float32) * scale
        p = jax.nn.softmax(s, axis=-1)
        return jnp.einsum('bhgqk,bhkd->bhgqd', p.astype(v.dtype), v)

    o = lax.map(attend, qb)
    return jnp.moveaxis(o, 0, 3).reshape(B, KH, G, S, Dv)


def depthwise_conv_centred(x, w):
    pad = (DN_CONV_W - 1) // 2
    return lax.conv_general_dilated(x, w[:, None, :].astype(x.dtype), window_strides=(1,),
                                    padding=[(pad, pad)], dimension_numbers=('NWC', 'WIO', 'NWC'),
                                    feature_group_count=x.shape[-1])


def gated_delta_rule(q, k, v, g, beta):
    out_dtype = v.dtype
    f32 = jnp.float32
    B, S, H, Dk = q.shape
    Dv = v.shape[-1]
    C = DN_CHUNK
    nc = S // C

    def chunks(t):
        t = t.astype(f32).reshape((B, nc, C, H) + t.shape[3:])
        return jnp.moveaxis(t, 3, 1)

    q = chunks(q) * (Dk ** -0.5)
    k = chunks(k)
    v = chunks(v)
    g = chunks(g)
    beta = chunks(beta)
    gc = jnp.cumsum(g, axis=-1)
    lower = jnp.tril(jnp.ones((C, C), dtype=bool))
    strict = jnp.tril(jnp.ones((C, C), dtype=bool), -1)
    decay = jnp.exp(jnp.where(lower, gc[..., :, None] - gc[..., None, :], -jnp.inf))
    kb = k * beta[..., None]
    vb = v * beta[..., None]
    lmat = jnp.where(strict, jnp.einsum('bhnid,bhnjd->bhnij', kb, k) * decay, 0.0)
    rhs = jnp.concatenate([vb, kb * jnp.exp(gc)[..., None]], axis=-1)
    sol = lax.linalg.triangular_solve(lmat + jnp.eye(C, dtype=f32), rhs, left_side=True,
                                      lower=True, unit_diagonal=True)
    u, w = sol[..., :Dv], sol[..., Dv:]
    a_qk = jnp.einsum('bhnid,bhnjd->bhnij', q, k) * decay
    g_last = gc[..., -1]
    k_tail = k * jnp.exp(g_last[..., None] - gc)[..., None]
    q_dec = q * jnp.exp(gc)[..., None]
    xs = tuple(jnp.moveaxis(t, 2, 0) for t in (q_dec, k_tail, u, w, a_qk, jnp.exp(g_last)))

    def step(state, inp):
        qd, kt, uc, wc, aq, dl = inp
        v_new = uc - jnp.einsum('bhck,bhkv->bhcv', wc, state)
        o = jnp.einsum('bhck,bhkv->bhcv', qd, state) + jnp.einsum('bhcj,bhjv->bhcv', aq, v_new)
        state = state * dl[..., None, None] + jnp.einsum('bhck,bhcv->bhkv', kt, v_new)
        return state, o

    state0 = jnp.zeros((B, H, Dk, Dv), f32)
    _, o = lax.scan(step, state0, xs)
    o = jnp.transpose(o, (1, 0, 3, 2, 4)).reshape(B, S, H, Dv)
    return o.astype(out_dtype)


def deltanet_group(qkv, z, b_raw, a_raw, conv_w, a_log, dt_bias, out_gain):
    B, S, _ = qkv.shape
    H = DN_HEADS
    f32 = jnp.float32
    qkv = jax.nn.silu(depthwise_conv_centred(qkv, conv_w))
    q, k, v = jnp.split(qkv, [H * DN_DK, 2 * H * DN_DK], axis=-1)
    q = l2_normalize(q.reshape(B, S, H, DN_DK))
    k = l2_normalize(k.reshape(B, S, H, DN_DK))
    v = v.reshape(B, S, H, DN_DV)
    beta = jax.nn.sigmoid(b_raw.astype(f32)).reshape(B, S, 2, H)
    g = -jnp.exp(a_log.astype(f32)) * jax.nn.softplus(a_raw.astype(f32).reshape(B, S, 2, H) + dt_bias.astype(f32))
    o_fwd = gated_delta_rule(q, k, v, g[:, :, 0], beta[:, :, 0])
    flip = lambda t: jnp.flip(t, axis=1)
    o_bwd = flip(gated_delta_rule(flip(q), flip(k), flip(v), flip(g[:, :, 1]), flip(beta[:, :, 1])))
    o = rms_norm(o_fwd + o_bwd, out_gain) * jax.nn.silu(z.reshape(B, S, H, DN_DV))
    return o.reshape(B, S, DN_WIDTH)


def gqa_group(q, k, v, q_gain, k_gain, cos, sin):
    B, S, _ = q.shape
    G = GQA_HEADS // GQA_KV_HEADS
    q = apply_rope(rms_norm(q.reshape(B, S, GQA_HEADS, GQA_HD), q_gain), cos, sin)
    k = apply_rope(rms_norm(k.reshape(B, S, GQA_KV_HEADS, GQA_HD), k_gain), cos, sin)
    v = v.reshape(B, S, GQA_KV_HEADS, GQA_HD)
    qh = q.reshape(B, S, GQA_KV_HEADS, G, GQA_HD).transpose(0, 2, 3, 1, 4)
    o = blocked_attention(qh, k.transpose(0, 2, 1, 3), v.transpose(0, 2, 1, 3), GQA_HD ** -0.5)
    return o.transpose(0, 3, 1, 2, 4).reshape(B, S, GQA_WIDTH)


def mla_group(cq, ckv, kr, q_lat_gain, kv_lat_gain, w_uq, w_ukv, qn_gain, qr_gain, kn_gain, kr_gain, cos, sin):
    B, S, _ = cq.shape
    H = MLA_HEADS
    q = (rms_norm(cq, q_lat_gain) @ w_uq).reshape(B, S, H, MLA_NOPE + MLA_ROPE)
    kv = (rms_norm(ckv, kv_lat_gain) @ w_ukv).reshape(B, S, H, MLA_NOPE + MLA_V)
    q_nope = rms_norm(q[..., :MLA_NOPE], qn_gain)
    q_rope = apply_rope(rms_norm(q[..., MLA_NOPE:], qr_gain), cos, sin)
    k_nope = rms_norm(kv[..., :MLA_NOPE], kn_gain)
    v = kv[..., MLA_NOPE:]
    k_rope = apply_rope(rms_norm(kr[:, :, None, :], kr_gain), cos, sin)
    qh = jnp.concatenate([q_nope, q_rope], axis=-1).transpose(0, 2, 1, 3)[:, :, None]
    kh = jnp.concatenate([k_nope, jnp.broadcast_to(k_rope, (B, S, H, MLA_ROPE))], axis=-1).transpose(0, 2, 1, 3)
    o = blocked_attention(qh, kh, v.transpose(0, 2, 1, 3), (MLA_NOPE + MLA_ROPE) ** -0.5)
    return o[:, :, 0].transpose(0, 2, 1, 3).reshape(B, S, MLA_WIDTH)


def parallel_head_mixer(h, w_in, dn_conv, dn_a_log, dn_dt_bias, dn_out_g, gqa_q_g, gqa_k_g,
                        mla_q_lat_g, mla_kv_lat_g, mla_w_uq, mla_w_ukv, mla_qn_g, mla_qr_g,
                        mla_kn_g, mla_kr_g, w_out, rope_gqa, rope_mla):
    proj = h @ w_in
    splits = np.cumsum(IN_SPLIT_SIZES)[:-1].tolist()
    dn_qkv, dn_z, dn_b, dn_a, gq, gk, gv, mcq, mckv, mkr = jnp.split(proj, splits, axis=-1)
    o_a = deltanet_group(dn_qkv, dn_z, dn_b, dn_a, dn_conv, dn_a_log, dn_dt_bias, dn_out_g)
    o_b = gqa_group(gq, gk, gv, gqa_q_g, gqa_k_g, rope_gqa[0], rope_gqa[1])
    o_c = mla_group(mcq, mckv, mkr, mla_q_lat_g, mla_kv_lat_g, mla_w_uq, mla_w_ukv,
                    mla_qn_g, mla_qr_g, mla_kn_g, mla_kr_g, rope_mla[0], rope_mla[1])
    return jnp.concatenate([o_a, o_b, o_c], axis=-1) @ w_out


def hierarchical_moe(h, w_group, b_group, w_router, b_router, w1, w3, w2):
    B, S, D = h.shape
    N = B * S
    f32 = jnp.float32
    xf = h.reshape(N, D)
    g_prob = jax.nn.softmax((xf @ w_group).astype(f32) + b_group.astype(f32), axis=-1)
    g_top_p, g_top = lax.top_k(g_prob, 1)
    e_logits = ((xf @ w_router).astype(f32) + b_router.astype(f32)).reshape(N, N_GROUPS, EXPERTS_PER_GROUP)
    e_sel = jnp.take_along_axis(e_logits, g_top[:, :, None], axis=1)[:, 0]
    e_top_p, e_top = lax.top_k(jax.nn.softmax(e_sel, axis=-1), TOP_K_IN_GROUP)
    e_top_p = e_top_p / jnp.sum(e_top_p, axis=-1, keepdims=True)
    weights = g_top_p * e_top_p
    expert_id = g_top * EXPERTS_PER_GROUP + e_top
    A = N * TOP_K_IN_GROUP
    e_flat = expert_id.reshape(A)
    tok_flat = jnp.repeat(jnp.arange(N, dtype=jnp.int32), TOP_K_IN_GROUP)
    w_flat = weights.reshape(A)
    order = jnp.argsort(e_flat)
    e_sorted = e_flat[order]
    counts = jnp.bincount(e_flat, length=N_EXPERTS)
    padded = ((counts + MOE_BLOCK - 1) // MOE_BLOCK) * MOE_BLOCK
    pad_end = jnp.cumsum(padded)
    pad_start = pad_end - padded
    start = jnp.cumsum(counts) - counts
    dest = pad_start[e_sorted] + (jnp.arange(A, dtype=jnp.int32) - start[e_sorted])
    n_blocks = (A + N_EXPERTS * (MOE_BLOCK - 1) + MOE_BLOCK - 1) // MOE_BLOCK
    P = n_blocks * MOE_BLOCK
    slot_tok = jnp.full((P,), N, dtype=jnp.int32).at[dest].set(tok_flat[order])
    slot_w = jnp.zeros((P,), f32).at[dest].set(w_flat[order])
    block_e = jnp.clip(jnp.searchsorted(pad_end, jnp.arange(n_blocks, dtype=jnp.int32) * MOE_BLOCK, side='right'),
                       0, N_EXPERTS - 1)
    xpad = jnp.concatenate([xf, jnp.zeros((1, D), xf.dtype)], axis=0)

    def expert_block(args):
        tok, e = args
        xb = xpad[tok]
        return (jax.nn.silu(xb @ w1[e]) * (xb @ w3[e])) @ w2[e]

    y = lax.map(expert_block, (slot_tok.reshape(n_blocks, MOE_BLOCK), block_e)).reshape(P, D)
    y = y * slot_w[:, None].astype(y.dtype)
    out = jnp.zeros((N + 1, D), y.dtype).at[slot_tok].add(y)[:N]
    return out.reshape(B, S, D)


def setup_inputs(seed: int = 0) -> dict:
    key = jax.random.key(seed)
    ks = iter(jax.random.split(key, 32))
    f32 = jnp.float32
    L, D = DEPTH, D_MODEL

    def nrm(shape, scale):
        return jax.random.normal(next(ks), shape, f32) * scale

    def gain(shape):
        return 1.0 + 0.02 * jax.random.normal(next(ks), shape, f32)

    x = nrm((BATCH, SEQ, D), 1.0)
    c = nrm((BATCH, D), 1.0)
    ada_w = nrm((L, D, 6 * D), 0.5 * D ** -0.5)
    ada_b = nrm((L, 6 * D), 0.02)
    norm1_g = gain((L, D))
    norm2_g = gain((L, D))
    w_in = nrm((L, D, IN_COLS), D ** -0.5)
    dn_conv = nrm((L, DN_CONV_W, 2 * DN_HEADS * DN_DK + DN_WIDTH), DN_CONV_W ** -0.5)
    dn_a_log = jnp.log(jax.random.uniform(next(ks), (L, 2, DN_HEADS), f32, 1.0, 16.0))
    dt = jnp.exp(jax.random.uniform(next(ks), (L, 2, DN_HEADS), f32, math.log(1e-3), math.log(1e-1)))
    dn_dt_bias = dt + jnp.log(-jnp.expm1(-dt))
    dn_out_g = gain((L, DN_DV))
    gqa_q_g = gain((L, GQA_HD))
    gqa_k_g = gain((L, GQA_HD))
    mla_q_lat_g = gain((L, MLA_Q_RANK))
    mla_kv_lat_g = gain((L, MLA_KV_RANK))
    mla_w_uq = nrm((L, MLA_Q_RANK, MLA_HEADS * (MLA_NOPE + MLA_ROPE)), MLA_Q_RANK ** -0.5)
    mla_w_ukv = nrm((L, MLA_KV_RANK, MLA_HEADS * (MLA_NOPE + MLA_V)), MLA_KV_RANK ** -0.5)
    mla_qn_g = gain((L, MLA_NOPE))
    mla_qr_g = gain((L, MLA_ROPE))
    mla_kn_g = gain((L, MLA_NOPE))
    mla_kr_g = gain((L, MLA_ROPE))
    w_out = nrm((L, D_MIX, D), D_MIX ** -0.5)
    moe_w_group = nrm((L, D, N_GROUPS), D ** -0.5)
    moe_b_group = nrm((L, N_GROUPS), 0.01)
    moe_w_router = nrm((L, D, N_EXPERTS), D ** -0.5)
    moe_b_router = nrm((L, N_EXPERTS), 0.01)
    moe_w1 = nrm((L, N_EXPERTS, D, D_EXPERT), D ** -0.5)
    moe_w3 = nrm((L, N_EXPERTS, D, D_EXPERT), D ** -0.5)
    moe_w2 = nrm((L, N_EXPERTS, D_EXPERT, D), D_EXPERT ** -0.5)
    return {"x": x, "c": c, "ada_w": ada_w, "ada_b": ada_b, "norm1_g": norm1_g, "norm2_g": norm2_g,
            "w_in": w_in, "dn_conv": dn_conv, "dn_a_log": dn_a_log, "dn_dt_bias": dn_dt_bias,
            "dn_out_g": dn_out_g, "gqa_q_g": gqa_q_g, "gqa_k_g": gqa_k_g,
            "mla_q_lat_g": mla_q_lat_g, "mla_kv_lat_g": mla_kv_lat_g, "mla_w_uq": mla_w_uq,
            "mla_w_ukv": mla_w_ukv, "mla_qn_g": mla_qn_g, "mla_qr_g": mla_qr_g, "mla_kn_g": mla_kn_g,
            "mla_kr_g": mla_kr_g, "w_out": w_out, "moe_w_group": moe_w_group, "moe_b_group": moe_b_group,
            "moe_w_router": moe_w_router, "moe_b_router": moe_b_router, "moe_w1": moe_w1,
            "moe_w3": moe_w3, "moe_w2": moe_w2}


def reference(x, c, ada_w, ada_b, norm1_g, norm2_g, w_in, dn_conv, dn_a_log, dn_dt_bias, dn_out_g,
              gqa_q_g, gqa_k_g, mla_q_lat_g, mla_kv_lat_g, mla_w_uq, mla_w_ukv, mla_qn_g, mla_qr_g,
              mla_kn_g, mla_kr_g, w_out, moe_w_group, moe_b_group, moe_w_router, moe_b_router,
              moe_w1, moe_w3, moe_w2):
    S = x.shape[1]
    rope_gqa = axial_rope_table(S, GQA_HD)
    rope_mla = axial_rope_table(S, MLA_ROPE)
    c_act = jax.nn.silu(c)
    for l in range(DEPTH):
        mod = (c_act @ ada_w[l] + ada_b[l])[:, None, :]
        sh1, sc1, gt1, sh2, sc2, gt2 = jnp.split(mod, 6, axis=-1)
        h = rms_norm(x, norm1_g[l]) * (1 + sc1) + sh1
        x = x + gt1 * parallel_head_mixer(h, w_in[l], dn_conv[l], dn_a_log[l], dn_dt_bias[l], dn_out_g[l],
                                          gqa_q_g[l], gqa_k_g[l], mla_q_lat_g[l], mla_kv_lat_g[l],
                                          mla_w_uq[l], mla_w_ukv[l], mla_qn_g[l], mla_qr_g[l],
                                          mla_kn_g[l], mla_kr_g[l], w_out[l], rope_gqa, rope_mla)
        h = rms_norm(x, norm2_g[l]) * (1 + sc2) + sh2
        x = x + gt2 * hierarchical_moe(h, moe_w_group[l], moe_b_group[l], moe_w_router[l], moe_b_router[l],
                                       moe_w1[l], moe_w3[l], moe_w2[l])
    return x
```

```python
import functools
import math

import jax
import jax.numpy as jnp
from jax import lax
from jax.experimental import pallas as pl
from jax.experimental.pallas import tpu as pltpu

F32 = jnp.float32
BF16 = jnp.bfloat16
HI = lax.Precision.HIGHEST

GRID_W = 64
EPS = 1e-6
ROPE_THETA = 10000.0
DN_H, DN_D, DN_C, DN_KW = 6, 64, 64, 5
DN_W = DN_H * DN_D
GQ_H, GQ_KV, GQ_D = 6, 2, 64
GQ_G = GQ_H // GQ_KV
GQ_W = GQ_H * GQ_D
ML_H, ML_QR, ML_KVR, ML_NOPE, ML_ROPE, ML_V = 4, 192, 128, 64, 32, 64
ML_W = ML_H * ML_V
N_GROUPS, E_PER_G, N_EXP, D_EXP = 4, 8, 32, 256
MOE_BLK = 128
LANE = 128

C_DNQKV, C_DNZ, C_GQ, C_GK, C_GV, C_MCKV, C_MCQ, C_MISC = 0, 1152, 1536, 1920, 2048, 2176, 2304, 2560
IN_COLS_P = 2688
MISC_B, MISC_A, MISC_KR = 0, 12, 64

TM_IN = 256
TM_PREP = 256
TM_DN = 256
TQ = 256
TK = 512
TM_OUT = 256
VMEM_LIMIT = 56 * 1024 * 1024


def _cp(sem, vmem=None):
    return pltpu.CompilerParams(dimension_semantics=sem, vmem_limit_bytes=vmem)


def _mm(a, b):
    return jnp.dot(a.astype(BF16), b.astype(BF16), preferred_element_type=F32)


def _mm32(a, b):
    return jnp.dot(a, b, preferred_element_type=F32, precision=HI)


def _sigmoid(x):
    return 1.0 / (1.0 + jnp.exp(-x))


def _gsum(x, g):
    hi = x.astype(BF16)
    lo = (x - hi.astype(F32)).astype(BF16)
    return (jnp.dot(hi, g, preferred_element_type=F32) + jnp.dot(lo, g, preferred_element_type=F32))


def _iota(shape, dim):
    return lax.broadcasted_iota(jnp.int32, shape, dim)


def _ada_kernel(c_ref, w_ref, b_ref, o_ref):
    c = c_ref[...]
    o_ref[0] = _mm32(c * _sigmoid(c), w_ref[0]) + b_ref[0]


def _ada_call(c, ada_w, ada_b):
    L, D, D6 = ada_w.shape
    B = c.shape[0]
    tn = 512
    return pl.pallas_call(
        _ada_kernel, grid=(L, D6 // tn),
        in_specs=[pl.BlockSpec((B, D), lambda l, j: (0, 0)),
                  pl.BlockSpec((1, D, tn), lambda l, j: (l, 0, j)),
                  pl.BlockSpec((1, 1, tn), lambda l, j: (l, 0, j))],
        out_specs=pl.BlockSpec((1, B, tn), lambda l, j: (l, 0, j)),
        out_shape=jax.ShapeDtypeStruct((L, B, D6), F32),
        compiler_params=_cp(("parallel", "parallel")), name="ada_mod",
    )(c, ada_w, ada_b.reshape(L, 1, D6))


def _inproj_kernel(*refs, has_acc):
    if has_acc:
        x_ref, acc_ref, modp_ref, mod_ref, g_ref, w_ref, xo_ref, p_ref = refs
        x = x_ref[0] + modp_ref[0, 5:6, :] * acc_ref[0]
        xo_ref[0] = x
    else:
        x_ref, mod_ref, g_ref, w_ref, p_ref = refs
        x = x_ref[0]
    ms = jnp.mean(x * x, axis=-1, keepdims=True)
    h = x * lax.rsqrt(ms + EPS) * g_ref[...]
    h = h * (1.0 + mod_ref[0, 1:2, :]) + mod_ref[0, 0:1, :]
    p_ref[0] = _mm(h, w_ref[...])


def _inproj_call(x, acc, modp, mod, g, w):
    B, S, D = x.shape
    tm = min(TM_IN, S)
    row = pl.BlockSpec((1, tm, D), lambda b, i: (b, i, 0))
    modspec = pl.BlockSpec((1, 6, D), lambda b, i: (b, 0, 0))
    has_acc = acc is not None
    in_specs = [row] + ([row, modspec] if has_acc else []) + [
        modspec, pl.BlockSpec((1, D), lambda b, i: (0, 0)),
        pl.BlockSpec((D, IN_COLS_P), lambda b, i: (0, 0))]
    pspec = pl.BlockSpec((1, tm, IN_COLS_P), lambda b, i: (b, i, 0))
    pshape = jax.ShapeDtypeStruct((B, S, IN_COLS_P), F32)
    args = (x,) + ((acc, modp) if has_acc else ()) + (mod, g, w)
    res = pl.pallas_call(
        functools.partial(_inproj_kernel, has_acc=has_acc), grid=(B, S // tm),
        in_specs=in_specs,
        out_specs=[row, pspec] if has_acc else pspec,
        out_shape=[jax.ShapeDtypeStruct((B, S, D), F32), pshape] if has_acc else pshape,
        compiler_params=_cp(("parallel", "parallel"), VMEM_LIMIT), name="inproj",
    )(*args)
    return res if has_acc else (x, res)


def _dn_prep_kernel(cur_ref, prev_ref, next_ref, misc_ref, cw_ref, gp_ref, g_ref,
                    q_ref, k_ref, v_ref, kT_ref, gcol_ref, grow_ref, xe_ref, *, tm):
    i = pl.program_id(1)
    n = pl.num_programs(1)
    xe_ref[0:8, :] = jnp.where(i > 0, prev_ref[0], 0.0)
    xe_ref[8:8 + tm, :] = cur_ref[0]
    xe_ref[8 + tm:16 + tm, :] = jnp.where(i < n - 1, next_ref[0], 0.0)
    y = cw_ref[0:1, :] * xe_ref[6:6 + tm, :]
    for j in range(1, DN_KW):
        y = y + cw_ref[j:j + 1, :] * xe_ref[6 + j:6 + j + tm, :]
    y = y * _sigmoid(y)
    q = y[:, 0:DN_W]
    k = y[:, DN_W:2 * DN_W]
    v = y[:, 2 * DN_W:3 * DN_W]
    g = g_ref[...]
    qn = q * lax.rsqrt(_gsum(q * q, g) + EPS)
    kn = k * lax.rsqrt(_gsum(k * k, g) + EPS)
    for h in range(DN_H):
        sl = slice(DN_D * h, DN_D * (h + 1))
        q_ref[0, h] = qn[:, sl]
        k_ref[0, h] = kn[:, sl]
        v_ref[0, h] = v[:, sl]
    for j in range(DN_W // LANE):
        t = kn[:, LANE * j:LANE * (j + 1)].T
        kT_ref[0, 2 * j] = t[0:DN_D]
        kT_ref[0, 2 * j + 1] = t[DN_D:2 * DN_D]
    m = misc_ref[0]
    beta = _sigmoid(m)
    z = m + gp_ref[1:2, :]
    sp = jnp.maximum(z, 0.0) + jnp.log1p(jnp.exp(-jnp.abs(z)))
    gg = -jnp.exp(gp_ref[0:1, :]) * sp
    lane = _iota(m.shape, 1)
    gates = jnp.where(lane < MISC_A, beta, jnp.where(lane < MISC_A + 2 * DN_H, gg, 0.0))
    gcol_ref[0] = gates
    grow_ref[0] = gates.T


def _dn_prep_call(proj, conv_w8, gparams, g384):
    B, S, _ = proj.shape
    tm = min(TM_PREP, S)
    r8 = tm // 8
    n8 = S // 8
    W = 3 * DN_W
    hs = jax.ShapeDtypeStruct((B, DN_H, S, DN_D), F32)
    hspec = pl.BlockSpec((1, DN_H, tm, DN_D), lambda b, i: (b, 0, i, 0))
    return pl.pallas_call(
        functools.partial(_dn_prep_kernel, tm=tm), grid=(B, S // tm),
        in_specs=[pl.BlockSpec((1, tm, W), lambda b, i: (b, i, 0)),
                  pl.BlockSpec((1, 8, W), lambda b, i: (b, jnp.maximum(i * r8 - 1, 0), 0)),
                  pl.BlockSpec((1, 8, W), lambda b, i: (b, jnp.minimum((i + 1) * r8, n8 - 1), 0)),
                  pl.BlockSpec((1, tm, LANE), lambda b, i: (b, i, C_MISC // LANE)),
                  pl.BlockSpec((8, W), lambda b, i: (0, 0)),
                  pl.BlockSpec((8, LANE), lambda b, i: (0, 0)),
                  pl.BlockSpec((DN_W, DN_W), lambda b, i: (0, 0))],
        out_specs=[hspec, hspec, hspec,
                   pl.BlockSpec((1, DN_H, DN_D, tm), lambda b, i: (b, 0, 0, i)),
                   pl.BlockSpec((1, tm, LANE), lambda b, i: (b, i, 0)),
                   pl.BlockSpec((1, LANE, tm), lambda b, i: (b, 0, i))],
        out_shape=[hs, hs, hs, jax.ShapeDtypeStruct((B, DN_H, DN_D, S), F32),
                   jax.ShapeDtypeStruct((B, S, LANE), F32), jax.ShapeDtypeStruct((B, LANE, S), F32)],
        scratch_shapes=[pltpu.VMEM((tm + 16, W), F32)],
        compiler_params=_cp(("parallel", "parallel"), VMEM_LIMIT), name="dn_prep",
    )(proj, proj, proj, proj, conv_w8, gparams, g384)


def _dn_intra_kernel(q_ref, k_ref, v_ref, kT_ref, gcol_ref, grow_ref, *out_refs, tm):
    h = pl.program_id(1)
    C = DN_C
    q = q_ref[0, 0]
    k = k_ref[0, 0]
    v = v_ref[0, 0]
    kT = kT_ref[0, 0]
    gcol = gcol_ref[0]
    grow = grow_ref[0]
    ri = _iota((tm, tm), 0)
    ci = _iota((tm, tm), 1)
    same = (ri // C) == (ci // C)
    tri_l = jnp.where(same & (ci <= ri), 1.0, 0.0)
    tri_u = jnp.where(same & (ci >= ri), 1.0, 0.0)
    ones_b = jnp.where(same, 1.0, 0.0)
    r64 = _iota((C, C), 0)
    c64 = _iota((C, C), 1)
    eye = jnp.where(r64 == c64, 1.0, 0.0)
    scale = DN_D ** -0.5

    def sel_b(idx):
        return jnp.where(_iota((LANE, C), 0) == idx, 1.0, 0.0)

    def sel_c(idx):
        return jnp.where(_iota((C, LANE), 1) == idx, 1.0, 0.0)

    for d in range(2):
        u_ref, w_ref, a_ref, qd_ref, e_ref, kt_ref = out_refs[6 * d:6 * d + 6]
        tri_col, tri_row = (tri_l, tri_u) if d == 0 else (tri_u, tri_l)
        b_idx = MISC_B + DN_H * d + h
        g_idx = MISC_A + DN_H * d + h
        bb = _mm32(gcol, sel_b(b_idx))
        gb = _mm32(gcol, sel_b(g_idx))
        gc = _mm32(tri_col, gb)
        gt = _mm32(ones_b, gb)
        rr = _mm32(sel_c(g_idx), grow)
        rc = _mm32(rr, tri_row)
        rt = _mm32(rr, ones_b)
        incl = (c64 <= r64) if d == 0 else (c64 >= r64)
        strict = (c64 < r64) if d == 0 else (c64 > r64)
        for c in range(tm // C):
            sl = slice(C * c, C * (c + 1))
            gc_c = gc[sl]
            rc_c = rc[:, sl]
            dm = jnp.where(incl, jnp.exp(jnp.where(incl, gc_c - rc_c, 0.0)), 0.0)
            kc = k[sl]
            qc = q[sl]
            kTc = kT[:, sl]
            kb = kc * bb[sl]
            vb = v[sl] * bb[sl]
            lm = jnp.where(strict, _mm(kb, kTc) * dm, 0.0)
            m = -lm
            t = eye + m
            for _ in range(5):
                m = _mm(m, m)
                t = t + _mm(t, m)
            egc = jnp.exp(gc_c)
            u_ref[0, 0, sl, :] = _mm(t, vb)
            w_ref[0, 0, sl, :] = _mm(t, kb * egc)
            a_ref[0, 0, sl, :] = jnp.where(incl, _mm(qc, kTc) * dm, 0.0) * scale
            qd_ref[0, 0, sl, :] = qc * egc * scale
            e_ref[0, 0, sl, :] = jnp.exp(gt[sl])
            kt_ref[0, 0, :, sl] = kTc * jnp.exp(rt[:, sl] - rc_c)


def _dn_intra_call(q, k, v, kT, gcol, grow):
    B, H, S, Dh = q.shape
    tm = min(TM_DN, S)
    hs = jax.ShapeDtypeStruct((B, H, S, Dh), F32)
    ts = jax.ShapeDtypeStruct((B, H, Dh, S), F32)
    hspec = pl.BlockSpec((1, 1, tm, Dh), lambda b, h, i: (b, h, i, 0))
    tspec = pl.BlockSpec((1, 1, Dh, tm), lambda b, h, i: (b, h, 0, i))
    one_dir_shapes = [hs, hs, hs, hs, hs, ts]
    one_dir_specs = [hspec, hspec, hspec, hspec, hspec, tspec]
    return pl.pallas_call(
        functools.partial(_dn_intra_kernel, tm=tm), grid=(B, H, S // tm),
        in_specs=[hspec, hspec, hspec, tspec,
                  pl.BlockSpec((1, tm, LANE), lambda b, h, i: (b, i, 0)),
                  pl.BlockSpec((1, LANE, tm), lambda b, h, i: (b, 0, i))],
        out_specs=one_dir_specs * 2, out_shape=one_dir_shapes * 2,
        compiler_params=_cp(("parallel", "parallel", "parallel"), VMEM_LIMIT), name="dn_intra",
    )(q, k, v, kT, gcol, grow)


def _dn_scan_kernel(*refs, tm):
    fwd = refs[0:6]
    bwd = refs[6:12]
    of_ref, ob_ref, s_ref = refs[12:15]
    C = DN_C
    nch = tm // C

    @pl.when(pl.program_id(1) == 0)
    def _():
        s_ref[...] = jnp.zeros(s_ref.shape, F32)

    for c in range(nch):
        for d, (ins, o_ref) in enumerate(((fwd, of_ref), (bwd, ob_ref))):
            u_ref, w_ref, a_ref, qd_ref, e_ref, kt_ref = ins
            cc = c if d == 0 else nch - 1 - c
            sl = slice(C * cc, C * (cc + 1))
            outs = []
            for h in range(DN_H):
                st = s_ref[DN_H * d + h]
                vnew = u_ref[0, h, sl, :] - _mm(w_ref[0, h, sl, :], st)
                outs.append(_mm(qd_ref[0, h, sl, :], st) + _mm(a_ref[0, h, sl, :], vnew))
                s_ref[DN_H * d + h] = st * e_ref[0, h, sl, :] + _mm(kt_ref[0, h, :, sl], vnew)
            for j in range(DN_H // 2):
                o_ref[0, sl, LANE * j:LANE * (j + 1)] = jnp.concatenate([outs[2 * j], outs[2 * j + 1]], axis=1)


def _dn_scan_call(intra):
    B, H, S, Dh = intra[0].shape
    tm = min(TM_DN, S)
    nb = S // tm
    def specs(rev):
        ix = (lambda i: nb - 1 - i) if rev else (lambda i: i)
        hspec = pl.BlockSpec((1, H, tm, Dh), lambda b, i: (b, 0, ix(i), 0))
        tspec = pl.BlockSpec((1, H, Dh, tm), lambda b, i: (b, 0, 0, ix(i)))
        return [hspec] * 5 + [tspec]
    os_ = jax.ShapeDtypeStruct((B, S, H * Dh), F32)
    return pl.pallas_call(
        functools.partial(_dn_scan_kernel, tm=tm), grid=(B, nb),
        in_specs=specs(False) + specs(True),
        out_specs=[pl.BlockSpec((1, tm, H * Dh), lambda b, i: (b, i, 0)),
                   pl.BlockSpec((1, tm, H * Dh), lambda b, i: (b, nb - 1 - i, 0))],
        out_shape=[os_, os_],
        scratch_shapes=[pltpu.VMEM((2 * H, Dh, Dh), F32)],
        compiler_params=_cp(("parallel", "arbitrary"), VMEM_LIMIT), name="dn_scan",
    )(*intra)


def _rope(y, cos, sin, half, period):
    w = y.shape[1]
    lane = _iota(y.shape, 1)
    first = (lane % period) < half if period == 2 * half else ((lane % period) >= ML_NOPE) & ((lane % period) < ML_NOPE + half)
    swapped = jnp.where(first, pltpu.roll(y, w - half, 1), pltpu.roll(y, half, 1))
    return y * cos + swapped * sin


def _gqa_prep_kernel(q_ref, k_ref, v_ref, qg_ref, kg_ref, cq_ref, sq_ref, ck_ref, sk_ref, gq_ref, gk_ref,
                     qo_ref, kT_ref, vo_ref):
    q = q_ref[0]
    qn = q * lax.rsqrt(_gsum(q * q, gq_ref[...]) * (1.0 / GQ_D) + EPS) * qg_ref[...]
    qr = _rope(qn, cq_ref[...], sq_ref[...], GQ_D // 2, GQ_D)
    qo_ref[0] = (qr * (GQ_D ** -0.5)).astype(BF16)
    k = k_ref[0]
    kn = k * lax.rsqrt(_gsum(k * k, gk_ref[...]) * (1.0 / GQ_D) + EPS) * kg_ref[...]
    kr = _rope(kn, ck_ref[...], sk_ref[...], GQ_D // 2, GQ_D)
    kT_ref[0] = kr.T.astype(BF16)
    v = v_ref[0]
    for h in range(GQ_KV):
        vo_ref[0, h] = v[:, GQ_D * h:GQ_D * (h + 1)].astype(BF16)


def _gqa_prep_call(proj, qg, kg, cos_q, sin_q, cos_k, sin_k, g384, g128):
    B, S, _ = proj.shape
    tm = min(TM_PREP, S)
    KW = GQ_KV * GQ_D
    const = lambda shape: pl.BlockSpec(shape, lambda b, i: (0, 0))
    tab = lambda w: pl.BlockSpec((tm, w), lambda b, i: (i, 0))
    return pl.pallas_call(
        _gqa_prep_kernel, grid=(B, S // tm),
        in_specs=[pl.BlockSpec((1, tm, GQ_W), lambda b, i: (b, i, C_GQ // GQ_W)),
                  pl.BlockSpec((1, tm, KW), lambda b, i: (b, i, C_GK // KW)),
                  pl.BlockSpec((1, tm, KW), lambda b, i: (b, i, C_GV // KW)),
                  const((1, GQ_W)), const((1, KW)), tab(GQ_W), tab(GQ_W), tab(KW), tab(KW),
                  const((GQ_W, GQ_W)), const((KW, KW))],
        out_specs=[pl.BlockSpec((1, tm, GQ_W), lambda b, i: (b, i, 0)),
                   pl.BlockSpec((1, KW, tm), lambda b, i: (b, 0, i)),
                   pl.BlockSpec((1, GQ_KV, tm, GQ_D), lambda b, i: (b, 0, i, 0))],
        out_shape=[jax.ShapeDtypeStruct((B, S, GQ_W), BF16), jax.ShapeDtypeStruct((B, KW, S), BF16),
                   jax.ShapeDtypeStruct((B, GQ_KV, S, GQ_D), BF16)],
        compiler_params=_cp(("parallel", "parallel"), VMEM_LIMIT), name="gqa_prep",
    )(proj, proj, proj, qg, kg, cos_q, sin_q, cos_k, sin_k, g384, g128)


def _mla_prep_kernel(cq_ref, ckv_ref, misc_ref, qlg_ref, kvlg_ref, wuq_ref, wuk_ref, wuv_ref, qg_ref, kg_ref,
                     inv_ref, cos_ref, sin_ref, gm_ref, qo_ref, kT_ref, vo_ref):
    W = ML_H * LANE
    cq = cq_ref[0]
    ms = jnp.sum(cq * cq, axis=-1, keepdims=True) * (1.0 / ML_QR)
    ql = cq * lax.rsqrt(ms + EPS) * qlg_ref[...]
    q = _mm(ql, wuq_ref[...])
    gm = gm_ref[...]
    inv = inv_ref[...]
    cos = cos_ref[...]
    sin = sin_ref[...]
    qn = q * lax.rsqrt(_gsum(q * q, gm) * inv + EPS) * qg_ref[...]
    qr = _rope(qn, cos, sin, ML_ROPE // 2, LANE)
    qo_ref[0] = (qr * ((ML_NOPE + ML_ROPE) ** -0.5)).astype(BF16)
    ckv = ckv_ref[0]
    ms = jnp.mean(ckv * ckv, axis=-1, keepdims=True)
    kvl = ckv * lax.rsqrt(ms + EPS) * kvlg_ref[...]
    kn = _mm(kvl, wuk_ref[...])
    misc = misc_ref[0]
    lane = _iota(kn.shape, 1) % LANE
    is_rope = (lane >= ML_NOPE) & (lane < ML_NOPE + ML_ROPE)
    kf = jnp.where(is_rope, jnp.concatenate([misc] * ML_H, axis=1), kn)
    kfn = kf * lax.rsqrt(_gsum(kf * kf, gm) * inv + EPS) * kg_ref[...]
    kr = _rope(kfn, cos, sin, ML_ROPE // 2, LANE)
    for h in range(ML_H):
        kT_ref[0, LANE * h:LANE * (h + 1), :] = kr[:, LANE * h:LANE * (h + 1)].T.astype(BF16)
    vv = _mm(kvl, wuv_ref[...])
    for h in range(ML_H):
        vo_ref[0, h] = vv[:, ML_V * h:ML_V * (h + 1)].astype(BF16)


def _mla_prep_call(proj, qlg, kvlg, wuq, wuk, wuv, qg, kg, inv, cos, sin, gm):
    B, S, _ = proj.shape
    tm = min(TM_PREP, S)
    W = ML_H * LANE
    QP = 256
    const = lambda shape: pl.BlockSpec(shape, lambda b, i: (0, 0))
    tab = pl.BlockSpec((tm, W), lambda b, i: (i, 0))
    return pl.pallas_call(
        _mla_prep_kernel, grid=(B, S // tm),
        in_specs=[pl.BlockSpec((1, tm, QP), lambda b, i: (b, i, C_MCQ // QP)),
                  pl.BlockSpec((1, tm, ML_KVR), lambda b, i: (b, i, C_MCKV // ML_KVR)),
                  pl.BlockSpec((1, tm, LANE), lambda b, i: (b, i, C_MISC // LANE)),
                  const((1, QP)), const((1, ML_KVR)), const((QP, W)), const((ML_KVR, W)), const((ML_KVR, ML_W)),
                  const((1, W)), const((1, W)), const((1, W)), tab, tab, const((W, W))],
        out_specs=[pl.BlockSpec((1, tm, W), lambda b, i: (b, i, 0)),
                   pl.BlockSpec((1, W, tm), lambda b, i: (b, 0, i)),
                   pl.BlockSpec((1, ML_H, tm, ML_V), lambda b, i: (b, 0, i, 0))],
        out_shape=[jax.ShapeDtypeStruct((B, S, W), BF16), jax.ShapeDtypeStruct((B, W, S), BF16),
                   jax.ShapeDtypeStruct((B, ML_H, S, ML_V), BF16)],
        compiler_params=_cp(("parallel", "parallel"), VMEM_LIMIT), name="mla_prep",
    )(proj, proj, proj, qlg, kvlg, wuq, wuk, wuv, qg, kg, inv, cos, sin, gm)


def _flash_rows(qs, kT_ref, v_ref, krows, vh, S, tk):
    R = qs.shape[0]
    dv = v_ref.shape[-1]

    def body(j, carry):
        m, l, acc = carry
        off = pl.multiple_of(j * tk, tk)
        s = jnp.dot(qs, kT_ref[0, krows, pl.ds(off, tk)], preferred_element_type=F32)
        mn = jnp.maximum(m, jnp.max(s, axis=-1, keepdims=True))
        alpha = jnp.exp(m - mn)
        p = jnp.exp(s - mn)
        l = alpha * l + jnp.sum(p, axis=-1, keepdims=True)
        acc = alpha * acc + jnp.dot(p.astype(BF16), v_ref[0, vh, pl.ds(off, tk), :], preferred_element_type=F32)
        return mn, l, acc

    init = (jnp.full((R, 1), -1e30, F32), jnp.zeros((R, 1), F32), jnp.zeros((R, dv), F32))
    m, l, acc = lax.fori_loop(0, S // tk, body, init)
    return acc / l


def _flash_gqa_kernel(q_ref, kT_ref, v_ref, o_ref, *, tq, tk, S):
    q = q_ref[0]
    outs = []
    for kh in range(GQ_KV):
        heads = [GQ_G * kh + g for g in range(GQ_G)]
        qs = jnp.concatenate([q[:, GQ_D * h:GQ_D * (h + 1)] for h in heads], axis=0)
        o = _flash_rows(qs, kT_ref, v_ref, slice(GQ_D * kh, GQ_D * (kh + 1)), kh, S, tk)
        outs += [o[tq * g:tq * (g + 1)] for g in range(GQ_G)]
    o_ref[0] = jnp.concatenate(outs, axis=1)


def _flash_mla_kernel(q_ref, kT_ref, v_ref, o_ref, *, tq, tk, S):
    q = q_ref[0]
    outs = []
    for h in range(ML_H):
        outs.append(_flash_rows(q[:, LANE * h:LANE * (h + 1)], kT_ref, v_ref,
                                slice(LANE * h, LANE * (h + 1)), h, S, tk))
    o_ref[0] = jnp.concatenate(outs, axis=1)


def _flash_call(kern, q, kT, v, out_w, name):
    B, S, QW = q.shape
    KR = kT.shape[1]
    NH, DV = v.shape[1], v.shape[3]
    tq = min(TQ, S)
    tk = min(TK, S)
    return pl.pallas_call(
        functools.partial(kern, tq=tq, tk=tk, S=S), grid=(B, S // tq),
        in_specs=[pl.BlockSpec((1, tq, QW), lambda b, i: (b, i, 0)),
                  pl.BlockSpec((1, KR, S), lambda b, i: (b, 0, 0)),
                  pl.BlockSpec((1, NH, S, DV), lambda b, i: (b, 0, 0, 0))],
        out_specs=pl.BlockSpec((1, tq, out_w), lambda b, i: (b, i, 0)),
        out_shape=jax.ShapeDtypeStruct((B, S, out_w), F32),
        compiler_params=_cp(("parallel", "parallel"), VMEM_LIMIT), name=name,
    )(q, kT, v)


def _outproj_kernel(of_ref, ob_ref, z_ref, og_ref, om_ref, x_ref, mod_ref, dng_ref, g_ref, wo_ref, n2_ref,
                    wr_ref, br_ref, xo_ref, h2_ref, rt_ref):
    o = of_ref[0] + ob_ref[0]
    ms = _gsum(o * o, g_ref[...]) * (1.0 / DN_D)
    z = z_ref[0]
    on = o * lax.rsqrt(ms + EPS) * dng_ref[...] * (z * _sigmoid(z))
    mix = (_mm(on, wo_ref[0:DN_W, :]) + _mm(og_ref[0], wo_ref[DN_W:DN_W + GQ_W, :])
           + _mm(om_ref[0], wo_ref[DN_W + GQ_W:DN_W + GQ_W + ML_W, :]))
    x = x_ref[0] + mod_ref[0, 2:3, :] * mix
    xo_ref[0] = x
    ms2 = jnp.mean(x * x, axis=-1, keepdims=True)
    h2 = x * lax.rsqrt(ms2 + EPS) * n2_ref[...]
    h2 = h2 * (1.0 + mod_ref[0, 4:5, :]) + mod_ref[0, 3:4, :]
    h2_ref[0] = h2
    logits = _mm32(h2, wr_ref[...]) + br_ref[...]
    lane = _iota(logits.shape, 1)
    lanef = lane.astype(F32)
    neg = -1e30
    is_g = lane < N_GROUPS
    gl = jnp.where(is_g, logits, neg)
    gmax = jnp.max(gl, axis=-1, keepdims=True)
    gsum = jnp.sum(jnp.where(is_g, jnp.exp(gl - gmax), 0.0), axis=-1, keepdims=True)
    g_top_p = 1.0 / gsum
    g_top = jnp.min(jnp.where(is_g & (gl == gmax), lanef, 1e3), axis=-1, keepdims=True)
    e_lo = 32.0 + E_PER_G * g_top
    in_grp = (lanef >= e_lo) & (lanef < e_lo + E_PER_G)
    el = jnp.where(in_grp, logits, neg)
    m1 = jnp.max(el, axis=-1, keepdims=True)
    i1 = jnp.min(jnp.where(in_grp & (el == m1), lanef, 1e3), axis=-1, keepdims=True)
    rest = in_grp & (lanef != i1)
    el2 = jnp.where(rest, logits, neg)
    m2 = jnp.max(el2, axis=-1, keepdims=True)
    i2 = jnp.min(jnp.where(rest & (el2 == m2), lanef, 1e3), axis=-1, keepdims=True)
    e21 = jnp.exp(m2 - m1)
    w1 = g_top_p / (1.0 + e21)
    w2 = w1 * e21
    rt_ref[0] = jnp.where(lane == 0, w1, jnp.where(lane == 1, w2, jnp.where(
        lane == 2, i1 - 32.0, jnp.where(lane == 3, i2 - 32.0, 0.0))))


def _outproj_call(o_f, o_b, proj, o_g, o_m, x, mod, dng, g384, wo, n2, wr, br):
    B, S, D = x.shape
    tm = min(TM_OUT, S)
    const = lambda shape: pl.BlockSpec(shape, lambda b, i: (0, 0))
    rowspec = lambda w: pl.BlockSpec((1, tm, w), lambda b, i: (b, i, 0))
    return pl.pallas_call(
        _outproj_kernel, grid=(B, S // tm),
        in_specs=[rowspec(DN_W), rowspec(DN_W),
                  pl.BlockSpec((1, tm, DN_W), lambda b, i: (b, i, C_DNZ // DN_W)),
                  rowspec(GQ_W), rowspec(ML_W), rowspec(D),
                  pl.BlockSpec((1, 6, D), lambda b, i: (b, 0, 0)),
                  const((1, DN_W)), const((DN_W, DN_W)), const((D, D)), const((1, D)),
                  const((D, LANE)), const((1, LANE))],
        out_specs=[rowspec(D), rowspec(D), rowspec(LANE)],
        out_shape=[jax.ShapeDtypeStruct((B, S, D), F32), jax.ShapeDtypeStruct((B, S, D), F32),
                   jax.ShapeDtypeStruct((B, S, LANE), F32)],
        compiler_params=_cp(("parallel", "parallel"), VMEM_LIMIT), name="outproj_router",
    )(o_f, o_b, proj, o_g, o_m, x, mod, dng, g384, wo, n2, wr, br)


def _moe_kernel(be_ref, nb_ref, tok_ref, sw_ref, h2_ref, w1_ref, w3_ref, w2_ref, acc_ref, xb_ref, yb_ref, *, nblk):
    b = pl.program_id(0)
    j = pl.program_id(1)

    @pl.when(j == 0)
    def _():
        acc_ref[...] = jnp.zeros(acc_ref.shape, F32)

    @pl.when(j < nb_ref[b])
    def _():
        base = (b * nblk + j) * MOE_BLK
        for r in range(MOE_BLK):
            t = tok_ref[base + r]
            xb_ref[r:r + 1, :] = h2_ref[0, pl.ds(t, 1), :]
        xb = xb_ref[...].astype(BF16)
        a = jnp.dot(xb, w1_ref[0], preferred_element_type=F32)
        g = jnp.dot(xb, w3_ref[0], preferred_element_type=F32)
        hm = (a * _sigmoid(a) * g).astype(BF16)
        yb_ref[...] = jnp.dot(hm, w2_ref[0], preferred_element_type=F32)
        for r in range(MOE_BLK):
            t = tok_ref[base + r]
            acc_ref[0, pl.ds(t, 1), :] = acc_ref[0, pl.ds(t, 1), :] + sw_ref[base + r] * yb_ref[r:r + 1, :]


def _moe_call(block_e, nblk_b, slot_tok, slot_w, h2, w1, w3, w2, nblk):
    B, S, D = h2.shape
    de = w1.shape[2]
    big = pl.Buffered(1)
    wmap = lambda b, j, be, nb, tok: (be[b * nblk + j], 0, 0)
    gs = pltpu.PrefetchScalarGridSpec(
        num_scalar_prefetch=3, grid=(B, nblk),
        in_specs=[pl.BlockSpec(memory_space=pltpu.SMEM),
                  pl.BlockSpec((1, S, D), lambda b, j, be, nb, tok: (b, 0, 0), pipeline_mode=big),
                  pl.BlockSpec((1, D, de), wmap), pl.BlockSpec((1, D, de), wmap), pl.BlockSpec((1, de, D), wmap)],
        out_specs=pl.BlockSpec((1, S, D), lambda b, j, be, nb, tok: (b, 0, 0), pipeline_mode=big),
        scratch_shapes=[pltpu.VMEM((MOE_BLK, D), F32), pltpu.VMEM((MOE_BLK, D), F32)])
    return pl.pallas_call(
        functools.partial(_moe_kernel, nblk=nblk), grid_spec=gs,
        out_shape=jax.ShapeDtypeStruct((B, S, D), F32),
        compiler_params=_cp(("parallel", "arbitrary"), VMEM_LIMIT), name="moe_experts",
    )(block_e, nblk_b, slot_tok, slot_w, h2, w1, w3, w2)


def _route_meta(route, nblk):
    B, S, _ = route.shape
    A = 2 * S
    P = nblk * MOE_BLK
    e_flat = route[:, :, 2:4].astype(jnp.int32).reshape(B, A)
    w_flat = route[:, :, 0:2].reshape(B, A)
    tok = jnp.arange(A, dtype=jnp.int32) // 2
    onehot = (e_flat[:, :, None] == jnp.arange(N_EXP, dtype=jnp.int32)).astype(jnp.int32)
    csum = jnp.cumsum(onehot, axis=1)
    rank = jnp.take_along_axis(csum, e_flat[:, :, None], axis=2)[:, :, 0] - 1
    counts = csum[:, -1, :]
    padded = ((counts + MOE_BLK - 1) // MOE_BLK) * MOE_BLK
    pad_end = jnp.cumsum(padded, axis=1)
    pad_start = pad_end - padded
    dest = jnp.take_along_axis(pad_start, e_flat, axis=1) + rank
    bidx = jnp.arange(B, dtype=jnp.int32)[:, None]
    slot_tok = jnp.zeros((B, P), jnp.int32).at[bidx, dest].set(jnp.broadcast_to(tok, (B, A)))
    slot_w = jnp.zeros((B, P), F32).at[bidx, dest].set(w_flat)
    nb = (pad_end[:, -1] // MOE_BLK).astype(jnp.int32)
    starts = jnp.minimum(jnp.arange(nblk, dtype=jnp.int32)[None, :], nb[:, None] - 1) * MOE_BLK
    block_e = jnp.sum((pad_end[:, None, :] <= starts[:, :, None]).astype(jnp.int32), axis=2)
    block_e = jnp.clip(block_e, 0, N_EXP - 1)
    return block_e.reshape(-1), nb, slot_tok.reshape(-1), slot_w.reshape(-1)


def _resid_kernel(x_ref, acc_ref, mod_ref, o_ref):
    o_ref[0] = x_ref[0] + mod_ref[0, 5:6, :] * acc_ref[0]


def _resid_call(x, acc, mod):
    B, S, D = x.shape
    tm = min(512, S)
    row = pl.BlockSpec((1, tm, D), lambda b, i: (b, i, 0))
    return pl.pallas_call(
        _resid_kernel, grid=(B, S // tm),
        in_specs=[row, row, pl.BlockSpec((1, 6, D), lambda b, i: (b, 0, 0))],
        out_specs=row, out_shape=jax.ShapeDtypeStruct((B, S, D), F32),
        compiler_params=_cp(("parallel", "parallel")), name="final_residual",
    )(x, acc, mod)


def _block_ones(width, group):
    i = jnp.arange(width)
    return (i[:, None] // group == i[None, :] // group).astype(BF16)


def _rope_tables(S):
    rows = S // GRID_W
    row = jnp.broadcast_to(jnp.arange(rows, dtype=F32)[:, None], (rows, GRID_W)).reshape(-1)
    col = jnp.broadcast_to(jnp.arange(GRID_W, dtype=F32)[None, :], (rows, GRID_W)).reshape(-1)

    def table(rot):
        nf = rot // 4
        inv = ROPE_THETA ** (-jnp.arange(nf, dtype=F32) / nf)
        ang = jnp.concatenate([row[:, None] * inv, col[:, None] * inv], axis=-1)
        return jnp.cos(ang), jnp.sin(ang)

    cg, sg = table(GQ_D)
    cos_g = jnp.concatenate([cg, cg], axis=1)
    sin_g = jnp.concatenate([-sg, sg], axis=1)
    cm, sm = table(ML_ROPE)
    one = jnp.ones((S, ML_NOPE), F32)
    zero = jnp.zeros((S, ML_NOPE), F32)
    pad1 = jnp.ones((S, LANE - ML_NOPE - ML_ROPE), F32)
    pad0 = jnp.zeros((S, LANE - ML_NOPE - ML_ROPE), F32)
    cos_m = jnp.concatenate([one, cm, cm, pad1], axis=1)
    sin_m = jnp.concatenate([zero, -sm, sm, pad0], axis=1)
    return (jnp.tile(cos_g, (1, GQ_H)), jnp.tile(sin_g, (1, GQ_H)), jnp.tile(cos_g, (1, GQ_KV)),
            jnp.tile(sin_g, (1, GQ_KV)), jnp.tile(cos_m, (1, ML_H)), jnp.tile(sin_m, (1, ML_H)))


def _layout_w_in(w):
    D = w.shape[0]
    o = 0
    parts = {}
    for name, n in (("dn_qkv", 3 * DN_W), ("dn_z", DN_W), ("dn_b", 2 * DN_H), ("dn_a", 2 * DN_H), ("gq", GQ_W),
                    ("gk", GQ_KV * GQ_D), ("gv", GQ_KV * GQ_D), ("mcq", ML_QR), ("mckv", ML_KVR), ("mkr", ML_ROPE)):
        parts[name] = w[:, o:o + n]
        o += n
    z = lambda n: jnp.zeros((D, n), w.dtype)
    misc = jnp.concatenate([parts["dn_b"], parts["dn_a"], z(MISC_KR - 4 * DN_H), parts["mkr"],
                            z(LANE - MISC_KR - ML_ROPE)], axis=1)
    return jnp.concatenate([parts["dn_qkv"], parts["dn_z"], parts["gq"], parts["gk"], parts["gv"], parts["mckv"],
                            parts["mcq"], z(256 - ML_QR), misc], axis=1).astype(BF16)


def _head_tiles(parts, pads):
    cols = []
    for h in range(ML_H):
        for p, n in zip(parts, pads):
            cols.append(p(h) if p is not None else jnp.zeros(n, F32))
    return cols


def kernel(x, c, ada_w, ada_b, norm1_g, norm2_g, w_in, dn_conv, dn_a_log, dn_dt_bias, dn_out_g, gqa_q_g, gqa_k_g,
           mla_q_lat_g, mla_kv_lat_g, mla_w_uq, mla_w_ukv, mla_qn_g, mla_qr_g, mla_kn_g, mla_kr_g, w_out,
           moe_w_group, moe_b_group, moe_w_router, moe_b_router, moe_w1, moe_w3, moe_w2):
    B, S, D = x.shape
    L = ada_w.shape[0]
    nblk = (2 * S + N_EXP * (MOE_BLK - 1) + MOE_BLK - 1) // MOE_BLK
    mods = _ada_call(c, ada_w, ada_b).reshape(L, B, 6, D)
    cos_q, sin_q, cos_k, sin_k, cos_m, sin_m = _rope_tables(S)
    g384 = _block_ones(DN_W, DN_D)
    g128 = _block_ones(GQ_KV * GQ_D, GQ_D)
    li = jnp.arange(ML_H * LANE) % LANE
    grp = jnp.where(li < ML_NOPE, 0, jnp.where(li < ML_NOPE + ML_ROPE, 1, 2 + li)) + 1000 * (jnp.arange(ML_H * LANE) // LANE)
    gm = (grp[:, None] == grp[None, :]).astype(BF16)
    inv_m = jnp.where(li < ML_NOPE, 1.0 / ML_NOPE, jnp.where(li < ML_NOPE + ML_ROPE, 1.0 / ML_ROPE, 1.0))[None, :].astype(F32)
    pad_t = LANE - ML_NOPE - ML_ROPE

    acc = None
    modp = None
    for l in range(L):
        mod = mods[l]
        w_in_p = _layout_w_in(w_in[l])
        x, proj = _inproj_call(x, acc, modp, mod, norm1_g[l][None, :], w_in_p)

        conv8 = jnp.concatenate([dn_conv[l], jnp.zeros((8 - DN_KW, 3 * DN_W), F32)], axis=0)
        gp = jnp.zeros((8, LANE), F32)
        gp = gp.at[0, MISC_A:MISC_A + 2 * DN_H].set(dn_a_log[l].reshape(-1))
        gp = gp.at[1, MISC_A:MISC_A + 2 * DN_H].set(dn_dt_bias[l].reshape(-1))
        dq, dk, dv, dkT, gcol, grow = _dn_prep_call(proj, conv8, gp, g384)
        o_f, o_b = _dn_scan_call(_dn_intra_call(dq, dk, dv, dkT, gcol, grow))

        gq, gkT, gv = _gqa_prep_call(proj, jnp.tile(gqa_q_g[l], GQ_H)[None, :], jnp.tile(gqa_k_g[l], GQ_KV)[None, :],
                                     cos_q, sin_q, cos_k, sin_k, g384, g128)
        o_g = _flash_call(_flash_gqa_kernel, gq, gkT, gv, GQ_W, "flash_gqa")

        wq = mla_w_uq[l].reshape(ML_QR, ML_H, ML_NOPE + ML_ROPE)
        wq = jnp.concatenate([wq, jnp.zeros((ML_QR, ML_H, pad_t), F32)], axis=2).reshape(ML_QR, ML_H * LANE)
        wq = jnp.concatenate([wq, jnp.zeros((256 - ML_QR, ML_H * LANE), F32)], axis=0).astype(BF16)
        wkv = mla_w_ukv[l].reshape(ML_KVR, ML_H, ML_NOPE + ML_V)
        wk = jnp.concatenate([wkv[:, :, :ML_NOPE], jnp.zeros((ML_KVR, ML_H, LANE - ML_NOPE), F32)], axis=2)
        wk = wk.reshape(ML_KVR, ML_H * LANE).astype(BF16)
        wv = wkv[:, :, ML_NOPE:].reshape(ML_KVR, ML_W).astype(BF16)
        zt = jnp.zeros((pad_t,), F32)
        qg_t = jnp.tile(jnp.concatenate([mla_qn_g[l], mla_qr_g[l], zt]), ML_H)[None, :]
        kg_t = jnp.tile(jnp.concatenate([mla_kn_g[l], mla_kr_g[l], zt]), ML_H)[None, :]
        qlg = jnp.concatenate([mla_q_lat_g[l], jnp.zeros((256 - ML_QR,), F32)])[None, :]
        mq, mkT, mv = _mla_prep_call(proj, qlg, mla_kv_lat_g[l][None, :], wq, wk, wv, qg_t, kg_t, inv_m,
                                     cos_m, sin_m, gm)
        o_m = _flash_call(_flash_mla_kernel, mq, mkT, mv, ML_W, "flash_mla")

        wr = jnp.zeros((D, LANE), F32).at[:, 0:N_GROUPS].set(moe_w_group[l]).at[:, 32:32 + N_EXP].set(moe_w_router[l])
        br = jnp.zeros((1, LANE), F32).at[0, 0:N_GROUPS].set(moe_b_group[l]).at[0, 32:32 + N_EXP].set(moe_b_router[l])
        x, h2, route = _outproj_call(o_f, o_b, proj, o_g, o_m, x, mod, jnp.tile(dn_out_g[l], DN_H)[None, :], g384,
                                     w_out[l].astype(BF16), norm2_g[l][None, :], wr, br)

        block_e, nb, slot_tok, slot_w = _route_meta(route, nblk)
        acc = _moe_call(block_e, nb, slot_tok, slot_w, h2, moe_w1[l].astype(BF16), moe_w3[l].astype(BF16),
                        moe_w2[l].astype(BF16), nblk)
        modp = mod
    return _resid_call(x, acc, modp)
```
